```python
import math
import jax, jax.numpy as jnp
from jax import lax
import numpy as np

D_MODEL = 1024
BATCH = 8
SEQ = 4096
DEPTH = 4

EXPAND = 2
BRANCH = EXPAND * D_MODEL
N_HEADS = 16
QK_HALF = 64
V_DIM = 2 * QK_HALF
QK_WIDTH = N_HEADS * 2 * QK_HALF
N_MIXERS = 2
CHUNK = 128
SGU_GROUPS = 16
SGU_GROUP_DIM = BRANCH // SGU_GROUPS
N_BUCKETS = 32
MAX_DISTANCE = 128
Q_BLOCK = 128
EPS = 1e-6
N_ATTN = (DEPTH + 1) // 2
N_SGU = DEPTH // 2

kernel_name = "hybrid_diffattn_chunked_sgu"


def rms_norm(x, g):
    xf = x.astype(jnp.float32)
    y = xf * lax.rsqrt(jnp.mean(xf * xf, axis=-1, keepdims=True) + EPS)
    return (y * g.astype(jnp.float32)).astype(x.dtype)


def t5_bucket(rel):
    n = jnp.maximum(-rel, 0)
    max_exact = N_BUCKETS // 2
    nf = jnp.maximum(n, 1).astype(jnp.float32)
    large = max_exact + (jnp.log(nf / max_exact) / math.log(MAX_DISTANCE / max_exact)
                         * (N_BUCKETS - max_exact)).astype(jnp.int32)
    large = jnp.minimum(large, N_BUCKETS - 1)
    return jnp.where(n < max_exact, n, large)


def diff_attention_branch(h, w_in, lq1, lk1, lq2, lk2, subln_g, rel_bias, lam_init):
    B, S, _ = h.shape
    proj = h @ w_in
    q = proj[..., :QK_WIDTH].reshape(B, S, N_HEADS, 2, QK_HALF)
    k = proj[..., QK_WIDTH:2 * QK_WIDTH].reshape(B, S, N_HEADS, 2, QK_HALF)
    v = proj[..., 2 * QK_WIDTH:2 * QK_WIDTH + BRANCH].reshape(B, S, N_HEADS, V_DIM)
    gate = proj[..., 2 * QK_WIDTH + BRANCH:]
    lam = (jnp.exp(jnp.sum(lq1.astype(jnp.float32) * lk1.astype(jnp.float32)))
           - jnp.exp(jnp.sum(lq2.astype(jnp.float32) * lk2.astype(jnp.float32))) + lam_init)
    q = q.transpose(3, 0, 2, 1, 4)
    k = k.transpose(3, 0, 2, 1, 4)
    v = v.transpose(0, 2, 1, 3)
    k1, k2 = k[0], k[1]
    nb = S // Q_BLOCK
    qb = q.reshape(2, B, N_HEADS, nb, Q_BLOCK, QK_HALF).transpose(3, 0, 1, 2, 4, 5)
    k_pos = jnp.arange(S, dtype=jnp.int32)
    scale = QK_HALF ** -0.5
    table = rel_bias.astype(jnp.float32)

    def block(args):
        i, q_i = args
        q_pos = i * Q_BLOCK + jnp.arange(Q_BLOCK, dtype=jnp.int32)
        rel = k_pos[None, :] - q_pos[:, None]
        bias = table[t5_bucket(rel)].transpose(2, 0, 1)
        causal = rel <= 0

        def probs(qq, kk):
            s = jnp.einsum('bhqd,bhkd->bhqk', qq, kk).astype(jnp.float32) * scale + bias
            s = jnp.where(causal, s, -jnp.inf)
            return jax.nn.softmax(s, axis=-1)

        a = probs(q_i[0], k1) - lam * probs(q_i[1], k2)
        return jnp.einsum('bhqk,bhkd->bhqd', a.astype(v.dtype), v)

    o = lax.map(block, (jnp.arange(nb, dtype=jnp.int32), qb))
    o = o.transpose(1, 0, 3, 2, 4).reshape(B, S, N_HEADS, V_DIM)
    o = rms_norm(o, subln_g) * (1.0 - lam_init)
    o = o.reshape(B, S, BRANCH)
    return o * jax.nn.silu(gate)


def spatial_gating_branch(h, w_in, v_gain, w_s, b_s):
    B, S, _ = h.shape
    proj = h @ w_in
    u = proj[..., :BRANCH]
    v = rms_norm(proj[..., BRANCH:2 * BRANCH], v_gain)
    gate = proj[..., 2 * BRANCH:]
    v = v.reshape(B, S // CHUNK, CHUNK, SGU_GROUPS, SGU_GROUP_DIM)
    causal = jnp.tril(jnp.ones((CHUNK, CHUNK), dtype=bool))
    w = jnp.where(causal[None], w_s, jnp.zeros((), w_s.dtype))
    y = jnp.einsum('gts,bcsgd->bctgd', w, v) + b_s.T[:, :, None]
    y = y.reshape(B, S, BRANCH)
    return u * y * jax.nn.silu(gate)


def setup_inputs(seed: int = 0) -> dict:
    key = jax.random.key(seed)
    ks = jax.random.split(key, 17)
    f32 = jnp.float32
    n = lambda k, shape, s: jax.random.normal(k, shape, f32) * s
    return {
        "x": n(ks[0], (BATCH, SEQ, D_MODEL), 1.0),
        "rel_bias": n(ks[1], (N_BUCKETS, N_HEADS), 0.5),
        "attn_norm": 1.0 + n(ks[2], (N_ATTN, D_MODEL), 0.02),
        "attn_w_in": n(ks[3], (N_ATTN, D_MODEL, 2 * QK_WIDTH + 2 * BRANCH), D_MODEL ** -0.5),
        "attn_lam_q1": n(ks[4], (N_ATTN, QK_HALF), 0.1),
        "attn_lam_k1": n(ks[5], (N_ATTN, QK_HALF), 0.1),
        "attn_lam_q2": n(ks[6], (N_ATTN, QK_HALF), 0.1),
        "attn_lam_k2": n(ks[7], (N_ATTN, QK_HALF), 0.1),
        "attn_subln": 1.0 + n(ks[8], (N_ATTN, V_DIM), 0.02),
        "attn_w_out": n(ks[9], (N_ATTN, BRANCH, D_MODEL), BRANCH ** -0.5),
        "sgu_norm": 1.0 + n(ks[10], (N_SGU, D_MODEL), 0.02),
        "sgu_w_in": n(ks[11], (N_SGU, D_MODEL, 3 * BRANCH), D_MODEL ** -0.5),
        "sgu_v_norm": 1.0 + n(ks[12], (N_SGU, BRANCH), 0.02),
        "sgu_w_s": n(ks[13], (N_SGU, SGU_GROUPS, CHUNK, CHUNK), CHUNK ** -0.5),
        "sgu_b_s": 1.0 + n(ks[14], (N_SGU, SGU_GROUPS, CHUNK), 0.02),
        "sgu_w_out": n(ks[15], (N_SGU, BRANCH, D_MODEL), BRANCH ** -0.5),
        "final_norm": 1.0 + n(ks[16], (D_MODEL,), 0.02),
    }


def reference(x, rel_bias, attn_norm, attn_w_in, attn_lam_q1, attn_lam_k1, attn_lam_q2,
              attn_lam_k2, attn_subln, attn_w_out, sgu_norm, sgu_w_in, sgu_v_norm, sgu_w_s,
              sgu_b_s, sgu_w_out, final_norm):
    for i in range(DEPTH):
        j = i // N_MIXERS
        if i % N_MIXERS == 0:
            lam_init = 0.8 - 0.6 * math.exp(-0.3 * i)
            h = rms_norm(x, attn_norm[j])
            y = diff_attention_branch(h, attn_w_in[j], attn_lam_q1[j], attn_lam_k1[j],
                                      attn_lam_q2[j], attn_lam_k2[j], attn_subln[j],
                                      rel_bias, lam_init)
            x = x + y @ attn_w_out[j]
        else:
            h = rms_norm(x, sgu_norm[j])
            y = spatial_gating_branch(h, sgu_w_in[j], sgu_v_norm[j], sgu_w_s[j], sgu_b_s[j])
            x = x + y @ sgu_w_out[j]
    return rms_norm(x, final_norm)
```

```python
import functools
import math

import jax
import jax.numpy as jnp
import numpy as np
from jax import lax
from jax.experimental import pallas as pl
from jax.experimental.pallas import tpu as pltpu

D_MODEL = 1024
N_HEADS = 16
QK_HALF = 64
HEAD_W = 2 * QK_HALF
BRANCH = N_HEADS * HEAD_W
CHUNK = 128
SGU_GROUPS = 16
N_BUCKETS = 32
MAX_EXACT = N_BUCKETS // 2
MAX_DISTANCE = 128
EPS = 1e-6
NEG = -1e30

V7X_VMEM_LIMIT = 56 * 1024 * 1024
SUB = 128

ATTN_BLOCK = 512
PROJ_ROWS = 256
OUT_ROWS = 512


def _t5_bucket_np(n):
    n = np.asarray(n, dtype=np.int32)
    nf = np.maximum(n, 1).astype(np.float32)
    ratio = np.log(nf / np.float32(MAX_EXACT)) / np.float32(math.log(MAX_DISTANCE / MAX_EXACT))
    large = MAX_EXACT + (ratio * np.float32(N_BUCKETS - MAX_EXACT)).astype(np.int32)
    large = np.minimum(large, N_BUCKETS - 1)
    return np.where(n < MAX_EXACT, n, large).astype(np.int32)


def _bucket_subtiles():
    r = np.arange(SUB, dtype=np.int32)[:, None]
    c = np.arange(SUB, dtype=np.int32)[None, :]
    diag = np.where(c - r >= 0, _t5_bucket_np(np.maximum(c - r, 0)), -1).astype(np.int32)
    nxt = _t5_bucket_np(SUB + c - r)
    assert (_t5_bucket_np(np.arange(SUB + 1, 8192)) == N_BUCKETS - 1).all()
    return diag, nxt


def _rms_rows(x, gain):
    return x * lax.rsqrt(jnp.mean(x * x, axis=-1, keepdims=True) + EPS) * gain


def _silu(x):
    return x * jax.nn.sigmoid(x)


def _bias_tile_kernel(tbl_ref, diag_ref, nxt_ref, out_ref):
    h = pl.program_id(0)
    diag_b = diag_ref[...]
    nxt_b = nxt_ref[...]
    e0 = jnp.zeros((SUB, SUB), jnp.float32)
    e1 = jnp.zeros((SUB, SUB), jnp.float32)
    for b in range(N_BUCKETS):
        t = tbl_ref[b, h]
        e0 = jnp.where(diag_b == b, t, e0)
        e1 = jnp.where(nxt_b == b, t, e1)
    e0 = jnp.where(diag_b < 0, NEG, e0)
    far = jnp.full((SUB, SUB), tbl_ref[N_BUCKETS - 1, h], jnp.float32)
    masked = jnp.full((SUB, SUB), NEG, jnp.float32)
    n_sub = ATTN_BLOCK // SUB
    for a in range(n_sub):
        for b in range(n_sub):
            for which, delta in ((0, b - a + n_sub), (1, b - a)):
                tile = masked if delta < 0 else e0 if delta == 0 else e1 if delta == 1 else far
                out_ref[0, which, a * SUB:(a + 1) * SUB, b * SUB:(b + 1) * SUB] = tile


def _bias_tiles(rel_bias):
    diag, nxt = _bucket_subtiles()
    return pl.pallas_call(
        _bias_tile_kernel,
        grid=(N_HEADS,),
        in_specs=[
            pl.BlockSpec(memory_space=pltpu.SMEM),
            pl.BlockSpec((SUB, SUB), lambda h: (0, 0)),
            pl.BlockSpec((SUB, SUB), lambda h: (0, 0)),
        ],
        out_specs=pl.BlockSpec((1, 2, ATTN_BLOCK, ATTN_BLOCK), lambda h: (h, 0, 0, 0)),
        out_shape=jax.ShapeDtypeStruct((N_HEADS, 2, ATTN_BLOCK, ATTN_BLOCK), jnp.float32),
        compiler_params=pltpu.CompilerParams(dimension_semantics=("arbitrary",)),
        name="bias_tiles",
    )(rel_bias, jnp.asarray(diag), jnp.asarray(nxt))


def _attn_proj_kernel(x_ref, g_ref, wqkg_ref, wvt_ref, q_ref, k_ref, gate_ref, vt_ref):
    h = _rms_rows(x_ref[...], g_ref[...]).astype(jnp.bfloat16)
    scale = QK_HALF ** -0.5
    for out_ref, col, mul in ((q_ref, 0, scale), (k_ref, BRANCH, None), (gate_ref, 2 * BRANCH, None)):
        for half in range(2):
            lo = half * (BRANCH // 2)
            acc = jnp.dot(h, wqkg_ref[:, col + lo:col + lo + BRANCH // 2],
                          preferred_element_type=jnp.float32)
            if mul is not None:
                acc = acc * mul
            out_ref[:, lo:lo + BRANCH // 2] = acc.astype(out_ref.dtype)
    vt = lax.dot_general(wvt_ref[...], h, (((1,), (1,)), ((), ())),
                         preferred_element_type=jnp.float32)
    vt_ref[...] = vt.astype(vt_ref.dtype)


def _attn_proj(x2d, gain, wqkg, wvt):
    tokens = x2d.shape[0]
    rows = PROJ_ROWS
    const = dict(pipeline_mode=pl.Buffered(1))
    out_bf16 = jax.ShapeDtypeStruct((tokens, BRANCH), jnp.bfloat16)
    return pl.pallas_call(
        _attn_proj_kernel,
        grid=(tokens // rows,),
        in_specs=[
            pl.BlockSpec((rows, D_MODEL), lambda i: (i, 0)),
            pl.BlockSpec((1, D_MODEL), lambda i: (0, 0)),
            pl.BlockSpec((D_MODEL, 3 * BRANCH), lambda i: (0, 0), **const),
            pl.BlockSpec((BRANCH, D_MODEL), lambda i: (0, 0), **const),
        ],
        out_specs=[
            pl.BlockSpec((rows, BRANCH), lambda i: (i, 0)),
            pl.BlockSpec((rows, BRANCH), lambda i: (i, 0)),
            pl.BlockSpec((rows, BRANCH), lambda i: (i, 0)),
            pl.BlockSpec((BRANCH, rows), lambda i: (0, i)),
        ],
        out_shape=[out_bf16, out_bf16, out_bf16,
                   jax.ShapeDtypeStruct((BRANCH, tokens), jnp.bfloat16)],
        compiler_params=pltpu.CompilerParams(dimension_semantics=("arbitrary",),
                                             vmem_limit_bytes=V7X_VMEM_LIMIT),
        name="attn_proj",
    )(x2d, gain, wqkg, wvt)


def _diff_attn_kernel(tbl_ref, lq1_ref, lk1_ref, lq2_ref, lk2_ref, subg_ref,
                      q_ref, k_ref, vt_ref, gate_ref, bias_ref, y_ref,
                      m_ref, l_ref, acc_ref, *, lam_init, seq):
    blk = ATTN_BLOCK
    h = pl.program_id(0)
    far_bias = tbl_ref[N_BUCKETS - 1, h]
    lam = (jnp.exp(jnp.sum(lq1_ref[...] * lk1_ref[...], axis=-1, keepdims=True))
           - jnp.exp(jnp.sum(lq2_ref[...] * lk2_ref[...], axis=-1, keepdims=True)) + lam_init)
    first_half = lax.broadcasted_iota(jnp.int32, (1, HEAD_W), 1) < QK_HALF

    def q_block(qi, carry):
        q0 = pl.multiple_of(qi * blk, blk)
        qb = q_ref[0, pl.ds(q0, blk), :]
        zero = jnp.zeros_like(qb)
        q_maps = (jnp.where(first_half, qb, zero), jnp.where(first_half, zero, qb))
        m_ref[...] = jnp.full(m_ref.shape, NEG, jnp.float32)
        l_ref[...] = jnp.zeros(l_ref.shape, jnp.float32)
        acc_ref[...] = jnp.zeros(acc_ref.shape, jnp.float32)

        def kv_step(k0, bias, shift):
            kb = k_ref[0, pl.ds(k0, blk), :]
            vtb = vt_ref[:, pl.ds(k0, blk)]
            for mp in range(2):
                s = lax.dot_general(kb, q_maps[mp], (((1,), (1,)), ((), ())),
                                    preferred_element_type=jnp.float32)
                if bias is not None:
                    s = s + bias
                m_old = m_ref[mp]
                m_new = jnp.maximum(m_old, jnp.max(s, axis=0, keepdims=True) + shift)
                alpha = jnp.exp(m_old - m_new)
                p = jnp.exp(s - (m_new - shift))
                l_ref[mp] = alpha * l_ref[mp] + jnp.sum(p, axis=0, keepdims=True)
                pv = jnp.dot(vtb, p.astype(jnp.bfloat16), preferred_element_type=jnp.float32)
                acc_ref[mp] = alpha * acc_ref[mp] + pv
                m_ref[mp] = m_new

        def far_step(j, c):
            kv_step(pl.multiple_of(j * blk, blk), None, far_bias)
            return c

        lax.fori_loop(0, jnp.maximum(qi - 1, 0), far_step, 0)

        @pl.when(qi >= 1)
        def _():
            kv_step(pl.multiple_of((qi - 1) * blk, blk), bias_ref[0, 0], 0.0)

        kv_step(q0, bias_ref[0, 1], 0.0)

        o_t = acc_ref[0] / l_ref[0] - lam * (acc_ref[1] / l_ref[1])
        o = o_t.T
        o = _rms_rows(o, subg_ref[...]) * (1.0 - lam_init)
        gate = gate_ref[0, pl.ds(q0, blk), :].astype(jnp.float32)
        y_ref[0, pl.ds(q0, blk), :] = (o * _silu(gate)).astype(y_ref.dtype)
        return carry

    lax.fori_loop(0, seq // blk, q_block, 0)


def _diff_attn(rel_bias, lq1, lk1, lq2, lk2, subg, q, k, vt, gate, bias, *, lam_init, batch, seq):
    blk = ATTN_BLOCK
    q3 = q.reshape(batch, seq, BRANCH)
    k3 = k.reshape(batch, seq, BRANCH)
    g3 = gate.reshape(batch, seq, BRANCH)
    vec = lambda n: pl.BlockSpec((1, n), lambda h, b: (0, 0))
    head = pl.BlockSpec((1, seq, HEAD_W), lambda h, b: (b, 0, h))
    y = pl.pallas_call(
        functools.partial(_diff_attn_kernel, lam_init=lam_init, seq=seq),
        grid=(N_HEADS, batch),
        in_specs=[
            pl.BlockSpec(memory_space=pltpu.SMEM),
            vec(QK_HALF), vec(QK_HALF), vec(QK_HALF), vec(QK_HALF), vec(HEAD_W),
            head, head,
            pl.BlockSpec((HEAD_W, seq), lambda h, b: (h, b)),
            head,
            pl.BlockSpec((1, 2, blk, blk), lambda h, b: (h, 0, 0, 0)),
        ],
        out_specs=head,
        out_shape=jax.ShapeDtypeStruct((batch, seq, BRANCH), jnp.bfloat16),
        scratch_shapes=[
            pltpu.VMEM((2, 1, blk), jnp.float32),
            pltpu.VMEM((2, 1, blk), jnp.float32),
            pltpu.VMEM((2, HEAD_W, blk), jnp.float32),
        ],
        compiler_params=pltpu.CompilerParams(dimension_semantics=("arbitrary", "arbitrary"),
                                             vmem_limit_bytes=V7X_VMEM_LIMIT),
        name="diff_attn",
    )(rel_bias, lq1, lk1, lq2, lk2, subg, q3, k3, vt, g3, bias)
    return y.reshape(batch * seq, BRANCH)


def _out_proj_kernel(x_ref, y_ref, w_ref, o_ref):
    o_ref[...] = x_ref[...] + jnp.dot(y_ref[...], w_ref[...], preferred_element_type=jnp.float32)


def _out_proj(x2d, y, w_out):
    tokens = x2d.shape[0]
    rows = OUT_ROWS
    return pl.pallas_call(
        _out_proj_kernel,
        grid=(tokens // rows,),
        in_specs=[
            pl.BlockSpec((rows, D_MODEL), lambda i: (i, 0)),
            pl.BlockSpec((rows, BRANCH), lambda i: (i, 0)),
            pl.BlockSpec((BRANCH, D_MODEL), lambda i: (0, 0), pipeline_mode=pl.Buffered(1)),
        ],
        out_specs=pl.BlockSpec((rows, D_MODEL), lambda i: (i, 0)),
        out_shape=jax.ShapeDtypeStruct((tokens, D_MODEL), jnp.float32),
        compiler_params=pltpu.CompilerParams(dimension_semantics=("arbitrary",),
                                             vmem_limit_bytes=V7X_VMEM_LIMIT),
        name="attn_out_proj",
    )(x2d, y, w_out)


def _sgu_kernel(x_ref, g_ref, win_ref, vgain_ref, ws_ref, bs_ref, wout_ref, fin_ref, o_ref,
                y_scr, *, final_norm):
    x = x_ref[...]
    rows = x.shape[0]
    h = _rms_rows(x, g_ref[...]).astype(jnp.bfloat16)
    u = jnp.dot(h, win_ref[:, :BRANCH], preferred_element_type=jnp.float32)
    v = jnp.dot(h, win_ref[:, BRANCH:2 * BRANCH], preferred_element_type=jnp.float32)
    gate = jnp.dot(h, win_ref[:, 2 * BRANCH:], preferred_element_type=jnp.float32)
    v = _rms_rows(v, vgain_ref[...]).astype(jnp.bfloat16)

    t_idx = lax.broadcasted_iota(jnp.int32, (CHUNK, CHUNK), 0)
    s_idx = lax.broadcasted_iota(jnp.int32, (CHUNK, CHUNK), 1)
    causal = s_idx <= t_idx
    gw = BRANCH // SGU_GROUPS
    for g in range(SGU_GROUPS):
        w = jnp.where(causal, ws_ref[g], 0.0).astype(jnp.bfloat16)
        b = bs_ref[:, g * gw:(g + 1) * gw]
        for c in range(rows // CHUNK):
            vc = v[c * CHUNK:(c + 1) * CHUNK, g * gw:(g + 1) * gw]
            y_scr[c * CHUNK:(c + 1) * CHUNK, g * gw:(g + 1) * gw] = (
                jnp.dot(w, vc, preferred_element_type=jnp.float32) + b)

    z = (u * y_scr[...] * _silu(gate)).astype(jnp.bfloat16)
    out = x + jnp.dot(z, wout_ref[...], preferred_element_type=jnp.float32)
    if final_norm:
        out = _rms_rows(out, fin_ref[...])
    o_ref[...] = out


def _sgu_layer(x2d, gain, w_in, v_gain, w_s, b_full, w_out, fin_gain, *, final_norm):
    tokens = x2d.shape[0]
    rows = PROJ_ROWS
    const = dict(pipeline_mode=pl.Buffered(1))
    return pl.pallas_call(
        functools.partial(_sgu_kernel, final_norm=final_norm),
        grid=(tokens // rows,),
        in_specs=[
            pl.BlockSpec((rows, D_MODEL), lambda i: (i, 0)),
            pl.BlockSpec((1, D_MODEL), lambda i: (0, 0)),
            pl.BlockSpec((D_MODEL, 3 * BRANCH), lambda i: (0, 0), **const),
            pl.BlockSpec((1, BRANCH), lambda i: (0, 0)),
            pl.BlockSpec((SGU_GROUPS, CHUNK, CHUNK), lambda i: (0, 0, 0), **const),
            pl.BlockSpec((CHUNK, BRANCH), lambda i: (0, 0), **const),
            pl.BlockSpec((BRANCH, D_MODEL), lambda i: (0, 0), **const),
            pl.BlockSpec((1, D_MODEL), lambda i: (0, 0)),
        ],
        out_specs=pl.BlockSpec((rows, D_MODEL), lambda i: (i, 0)),
        out_shape=jax.ShapeDtypeStruct((tokens, D_MODEL), jnp.float32),
        scratch_shapes=[pltpu.VMEM((rows, BRANCH), jnp.float32)],
        compiler_params=pltpu.CompilerParams(dimension_semantics=("arbitrary",),
                                             vmem_limit_bytes=V7X_VMEM_LIMIT),
        name="sgu_layer",
    )(x2d, gain, w_in, v_gain, w_s, b_full, w_out, fin_gain)


def kernel(x, rel_bias, attn_norm, attn_w_in, attn_lam_q1, attn_lam_k1, attn_lam_q2, attn_lam_k2,
           attn_subln, attn_w_out, sgu_norm, sgu_w_in, sgu_v_norm, sgu_w_s, sgu_b_s, sgu_w_out,
           final_norm):
    batch, seq, _ = x.shape
    assert seq % ATTN_BLOCK == 0 and (batch * seq) % OUT_ROWS == 0
    depth = attn_w_in.shape[0] + sgu_w_in.shape[0]
    bf16 = jnp.bfloat16
    x2d = x.reshape(batch * seq, D_MODEL)
    bias = _bias_tiles(rel_bias)
    row = lambda a: a.reshape(1, -1)
    fin = row(final_norm)

    for i in range(depth):
        j = i // 2
        if i % 2 == 0:
            lam_init = 0.8 - 0.6 * math.exp(-0.3 * i)
            w_in = attn_w_in[j]
            wqkg = jnp.concatenate([w_in[:, :2 * BRANCH], w_in[:, 3 * BRANCH:]], axis=1).astype(bf16)
            wvt = w_in[:, 2 * BRANCH:3 * BRANCH].T.astype(bf16)
            q, k, gate, vt = _attn_proj(x2d, row(attn_norm[j]), wqkg, wvt)
            y = _diff_attn(rel_bias, row(attn_lam_q1[j]), row(attn_lam_k1[j]), row(attn_lam_q2[j]),
                           row(attn_lam_k2[j]), row(attn_subln[j]), q, k, vt, gate, bias,
                           lam_init=lam_init, batch=batch, seq=seq)
            x2d = _out_proj(x2d, y, attn_w_out[j].astype(bf16))
        else:
            b_full = jnp.repeat(sgu_b_s[j].T, BRANCH // SGU_GROUPS, axis=1)
            x2d = _sgu_layer(x2d, row(sgu_norm[j]), sgu_w_in[j].astype(bf16), row(sgu_v_norm[j]),
                             sgu_w_s[j], b_full, sgu_w_out[j].astype(bf16), fin,
                             final_norm=(i == depth - 1))
    return x2d.reshape(batch, seq, D_MODEL)
```

```python
import functools
import math

import jax
import jax.numpy as jnp
import numpy as np
from jax import lax
from jax.experimental import pallas as pl
from jax.experimental.pallas import tpu as pltpu

D_MODEL = 1024
N_HEADS = 16
QK_HALF = 64
HEAD_W = 2 * QK_HALF
BRANCH = N_HEADS * HEAD_W
CHUNK = 128
SGU_GROUPS = 16
N_BUCKETS = 32
MAX_EXACT = N_BUCKETS // 2
MAX_DISTANCE = 128
EPS = 1e-6
NEG = -1e30

V7X_VMEM_LIMIT = 56 * 1024 * 1024
SUB = 128

ATTN_BLOCK = 512
PROJ_ROWS = 256
OUT_ROWS = 512


def _t5_bucket_np(n):
    n = np.asarray(n, dtype=np.int32)
    nf = np.maximum(n, 1).astype(np.float32)
    ratio = np.log(nf / np.float32(MAX_EXACT)) / np.float32(math.log(MAX_DISTANCE / MAX_EXACT))
    large = MAX_EXACT + (ratio * np.float32(N_BUCKETS - MAX_EXACT)).astype(np.int32)
    large = np.minimum(large, N_BUCKETS - 1)
    return np.where(n < MAX_EXACT, n, large).astype(np.int32)


def _bucket_subtiles():
    r = np.arange(SUB, dtype=np.int32)[:, None]
    c = np.arange(SUB, dtype=np.int32)[None, :]
    diag = np.where(c - r >= 0, _t5_bucket_np(np.maximum(c - r, 0)), -1).astype(np.int32)
    nxt = _t5_bucket_np(SUB + c - r)
    assert (_t5_bucket_np(np.arange(SUB + 1, 8192)) == N_BUCKETS - 1).all()
    return diag, nxt


def _rms_rows(x, gain):
    return x * lax.rsqrt(jnp.mean(x * x, axis=-1, keepdims=True) + EPS) * gain


def _silu(x):
    return x * jax.nn.sigmoid(x)


def _bias_tile_kernel(tbl_ref, diag_ref, nxt_ref, out_ref):
    h = pl.program_id(0)
    diag_b = diag_ref[...]
    nxt_b = nxt_ref[...]
    e0 = jnp.zeros((SUB, SUB), jnp.float32)
    e1 = jnp.zeros((SUB, SUB), jnp.float32)
    for b in range(N_BUCKETS):
        t = tbl_ref[b, h]
        e0 = jnp.where(diag_b == b, t, e0)
        e1 = jnp.where(nxt_b == b, t, e1)
    e0 = jnp.where(diag_b < 0, NEG, e0)
    far = jnp.full((SUB, SUB), tbl_ref[N_BUCKETS - 1, h], jnp.float32)
    masked = jnp.full((SUB, SUB), NEG, jnp.float32)
    n_sub = ATTN_BLOCK // SUB
    for a in range(n_sub):
        for b in range(n_sub):
            for which, delta in ((0, b - a + n_sub), (1, b - a)):
                tile = masked if delta < 0 else e0 if delta == 0 else e1 if delta == 1 else far
                out_ref[0, which, a * SUB:(a + 1) * SUB, b * SUB:(b + 1) * SUB] = tile


def _bias_tiles(rel_bias):
    diag, nxt = _bucket_subtiles()
    return pl.pallas_call(
        _bias_tile_kernel,
        grid=(N_HEADS,),
        in_specs=[
            pl.BlockSpec(memory_space=pltpu.SMEM),
            pl.BlockSpec((SUB, SUB), lambda h: (0, 0)),
            pl.BlockSpec((SUB, SUB), lambda h: (0, 0)),
        ],
        out_specs=pl.BlockSpec((1, 2, ATTN_BLOCK, ATTN_BLOCK), lambda h: (h, 0, 0, 0)),
        out_shape=jax.ShapeDtypeStruct((N_HEADS, 2, ATTN_BLOCK, ATTN_BLOCK), jnp.float32),
        compiler_params=pltpu.CompilerParams(dimension_semantics=("arbitrary",)),
        name="bias_tiles",
    )(rel_bias, jnp.asarray(diag), jnp.asarray(nxt))


def _attn_proj_kernel(x_ref, g_ref, wqkg_ref, wvt_ref, q_ref, k_ref, gate_ref, vt_ref):
    h = _rms_rows(x_ref[...], g_ref[...]).astype(jnp.bfloat16)
    scale = QK_HALF ** -0.5
    for out_ref, col, mul in ((q_ref, 0, scale), (k_ref, BRANCH, None), (gate_ref, 2 * BRANCH, None)):
        for half in range(2):
            lo = half * (BRANCH // 2)
            acc = jnp.dot(h, wqkg_ref[:, col + lo:col + lo + BRANCH // 2],
                          preferred_element_type=jnp.float32)
            if mul is not None:
                acc = acc * mul
            out_ref[:, lo:lo + BRANCH // 2] = acc.astype(out_ref.dtype)
    vt = lax.dot_general(wvt_ref[...], h, (((1,), (1,)), ((), ())),
                         preferred_element_type=jnp.float32)
    vt_ref[...] = vt.astype(vt_ref.dtype)


def _attn_proj(x2d, gain, wqkg, wvt):
    tokens = x2d.shape[0]
    rows = PROJ_ROWS
    const = dict(pipeline_mode=pl.Buffered(1))
    out_bf16 = jax.ShapeDtypeStruct((tokens, BRANCH), jnp.bfloat16)
    return pl.pallas_call(
        _attn_proj_kernel,
        grid=(tokens // rows,),
        in_specs=[
            pl.BlockSpec((rows, D_MODEL), lambda i: (i, 0)),
            pl.BlockSpec((1, D_MODEL), lambda i: (0, 0)),
            pl.BlockSpec((D_MODEL, 3 * BRANCH), lambda i: (0, 0), **const),
            pl.BlockSpec((BRANCH, D_MODEL), lambda i: (0, 0), **const),
        ],
        out_specs=[
            pl.BlockSpec((rows, BRANCH), lambda i: (i, 0)),
            pl.BlockSpec((rows, BRANCH), lambda i: (i, 0)),
            pl.BlockSpec((rows, BRANCH), lambda i: (i, 0)),
            pl.BlockSpec((BRANCH, rows), lambda i: (0, i)),
        ],
        out_shape=[out_bf16, out_bf16, out_bf16,
                   jax.ShapeDtypeStruct((BRANCH, tokens), jnp.bfloat16)],
        compiler_params=pltpu.CompilerParams(dimension_semantics=("arbitrary",),
                                             vmem_limit_bytes=V7X_VMEM_LIMIT),
        name="attn_proj",
    )(x2d, gain, wqkg, wvt)


def _diff_attn_kernel(tbl_ref, lq1_ref, lk1_ref, lq2_ref, lk2_ref, subg_ref,
                      q_ref, k_ref, vt_ref, gate_ref, bias_ref, y_ref,
                      m_ref, l_ref, acc_ref, s0_ref, *, lam_init, seq):
    blk = ATTN_BLOCK
    h = pl.program_id(0)
    far_bias = tbl_ref[N_BUCKETS - 1, h]
    lam = (jnp.exp(jnp.sum(lq1_ref[...] * lk1_ref[...], axis=-1, keepdims=True))
           - jnp.exp(jnp.sum(lq2_ref[...] * lk2_ref[...], axis=-1, keepdims=True)) + lam_init)
    first_half = lax.broadcasted_iota(jnp.int32, (1, HEAD_W), 1) < QK_HALF

    def q_block(qi, carry):
        q0 = pl.multiple_of(qi * blk, blk)
        qb = q_ref[0, pl.ds(q0, blk), :]
        zero = jnp.zeros_like(qb)
        q_maps = (jnp.where(first_half, qb, zero), jnp.where(first_half, zero, qb))
        m_ref[...] = jnp.full(m_ref.shape, NEG, jnp.float32)
        l_ref[...] = jnp.zeros(l_ref.shape, jnp.float32)
        acc_ref[...] = jnp.zeros(acc_ref.shape, jnp.float32)

        def scores(mp, k0):
            kb = k_ref[0, pl.ds(k0, blk), :]
            return lax.dot_general(kb, q_maps[mp], (((1,), (1,)), ((), ())),
                                   preferred_element_type=jnp.float32)

        def accumulate(mp, s, vtb, bias, shift):
            if bias is not None:
                s = s + bias
            m_old = m_ref[mp]
            m_new = jnp.maximum(m_old, jnp.max(s, axis=0, keepdims=True) + shift)
            alpha = jnp.exp(m_old - m_new)
            p = jnp.exp(s - (m_new - shift))
            l_ref[mp] = alpha * l_ref[mp] + jnp.sum(p, axis=0, keepdims=True)
            pv = jnp.dot(vtb, p.astype(jnp.bfloat16), preferred_element_type=jnp.float32)
            acc_ref[mp] = alpha * acc_ref[mp] + pv
            m_ref[mp] = m_new

        def kv_step(k0, bias, shift, next_k0):
            vtb = vt_ref[:, pl.ds(k0, blk)]
            s1 = scores(1, k0)
            accumulate(0, s0_ref[...], vtb, bias, shift)
            if next_k0 is not None:
                s0_ref[...] = scores(0, next_k0)
            accumulate(1, s1, vtb, bias, shift)

        s0_ref[...] = scores(0, 0)

        def far_step(j, c):
            k0 = pl.multiple_of(j * blk, blk)
            kv_step(k0, None, far_bias, k0 + blk)
            return c

        lax.fori_loop(0, jnp.maximum(qi - 1, 0), far_step, 0)

        @pl.when(qi >= 1)
        def _():
            kv_step(pl.multiple_of((qi - 1) * blk, blk), bias_ref[0, 0], 0.0, q0)

        kv_step(q0, bias_ref[0, 1], 0.0, None)

        o_t = acc_ref[0] / l_ref[0] - lam * (acc_ref[1] / l_ref[1])
        o = o_t.T
        o = _rms_rows(o, subg_ref[...]) * (1.0 - lam_init)
        gate = gate_ref[0, pl.ds(q0, blk), :].astype(jnp.float32)
        y_ref[0, pl.ds(q0, blk), :] = (o * _silu(gate)).astype(y_ref.dtype)
        return carry

    lax.fori_loop(0, seq // blk, q_block, 0)


def _diff_attn(rel_bias, lq1, lk1, lq2, lk2, subg, q, k, vt, gate, bias, *, lam_init, batch, seq):
    blk = ATTN_BLOCK
    q3 = q.reshape(batch, seq, BRANCH)
    k3 = k.reshape(batch, seq, BRANCH)
    g3 = gate.reshape(batch, seq, BRANCH)
    vec = lambda n: pl.BlockSpec((1, n), lambda h, b: (0, 0))
    head = pl.BlockSpec((1, seq, HEAD_W), lambda h, b: (b, 0, h))
    y = pl.pallas_call(
        functools.partial(_diff_attn_kernel, lam_init=lam_init, seq=seq),
        grid=(N_HEADS, batch),
        in_specs=[
            pl.BlockSpec(memory_space=pltpu.SMEM),
            vec(QK_HALF), vec(QK_HALF), vec(QK_HALF), vec(QK_HALF), vec(HEAD_W),
            head, head,
            pl.BlockSpec((HEAD_W, seq), lambda h, b: (h, b)),
            head,
            pl.BlockSpec((1, 2, blk, blk), lambda h, b: (h, 0, 0, 0)),
        ],
        out_specs=head,
        out_shape=jax.ShapeDtypeStruct((batch, seq, BRANCH), jnp.bfloat16),
        scratch_shapes=[
            pltpu.VMEM((2, 1, blk), jnp.float32),
            pltpu.VMEM((2, 1, blk), jnp.float32),
            pltpu.VMEM((2, HEAD_W, blk), jnp.float32),
            pltpu.VMEM((blk, blk), jnp.float32),
        ],
        compiler_params=pltpu.CompilerParams(dimension_semantics=("arbitrary", "arbitrary"),
                                             vmem_limit_bytes=V7X_VMEM_LIMIT),
        name="diff_attn",
    )(rel_bias, lq1, lk1, lq2, lk2, subg, q3, k3, vt, g3, bias)
    return y.reshape(batch * seq, BRANCH)


def _out_proj_kernel(x_ref, y_ref, w_ref, o_ref):
    o_ref[...] = x_ref[...] + jnp.dot(y_ref[...], w_ref[...], preferred_element_type=jnp.float32)


def _out_proj(x2d, y, w_out):
    tokens = x2d.shape[0]
    rows = OUT_ROWS
    return pl.pallas_call(
        _out_proj_kernel,
        grid=(tokens // rows,),
        in_specs=[
            pl.BlockSpec((rows, D_MODEL), lambda i: (i, 0)),
            pl.BlockSpec((rows, BRANCH), lambda i: (i, 0)),
            pl.BlockSpec((BRANCH, D_MODEL), lambda i: (0, 0), pipeline_mode=pl.Buffered(1)),
        ],
        out_specs=pl.BlockSpec((rows, D_MODEL), lambda i: (i, 0)),
        out_shape=jax.ShapeDtypeStruct((tokens, D_MODEL), jnp.float32),
        compiler_params=pltpu.CompilerParams(dimension_semantics=("arbitrary",),
                                             vmem_limit_bytes=V7X_VMEM_LIMIT),
        name="attn_out_proj",
    )(x2d, y, w_out)


def _sgu_kernel(x_ref, g_ref, win_ref, vgain_ref, ws_ref, bs_ref, wout_ref, fin_ref, o_ref,
                y_scr, *, final_norm):
    x = x_ref[...]
    rows = x.shape[0]
    h = _rms_rows(x, g_ref[...]).astype(jnp.bfloat16)
    u = jnp.dot(h, win_ref[:, :BRANCH], preferred_element_type=jnp.float32)
    v = jnp.dot(h, win_ref[:, BRANCH:2 * BRANCH], preferred_element_type=jnp.float32)
    gate = jnp.dot(h, win_ref[:, 2 * BRANCH:], preferred_element_type=jnp.float32)
    v = _rms_rows(v, vgain_ref[...]).astype(jnp.bfloat16)

    t_idx = lax.broadcasted_iota(jnp.int32, (CHUNK, CHUNK), 0)
    s_idx = lax.broadcasted_iota(jnp.int32, (CHUNK, CHUNK), 1)
    causal = s_idx <= t_idx
    gw = BRANCH // SGU_GROUPS
    for g in range(SGU_GROUPS):
        w = jnp.where(causal, ws_ref[g], 0.0).astype(jnp.bfloat16)
        b = bs_ref[:, g * gw:(g + 1) * gw]
        for c in range(rows // CHUNK):
            vc = v[c * CHUNK:(c + 1) * CHUNK, g * gw:(g + 1) * gw]
            y_scr[c * CHUNK:(c + 1) * CHUNK, g * gw:(g + 1) * gw] = (
                jnp.dot(w, vc, preferred_element_type=jnp.float32) + b)

    z = (u * y_scr[...] * _silu(gate)).astype(jnp.bfloat16)
    out = x + jnp.dot(z, wout_ref[...], preferred_element_type=jnp.float32)
    if final_norm:
        out = _rms_rows(out, fin_ref[...])
    o_ref[...] = out


def _sgu_layer(x2d, gain, w_in, v_gain, w_s, b_full, w_out, fin_gain, *, final_norm):
    tokens = x2d.shape[0]
    rows = PROJ_ROWS
    const = dict(pipeline_mode=pl.Buffered(1))
    return pl.pallas_call(
        functools.partial(_sgu_kernel, final_norm=final_norm),
        grid=(tokens // rows,),
        in_specs=[
            pl.BlockSpec((rows, D_MODEL), lambda i: (i, 0)),
            pl.BlockSpec((1, D_MODEL), lambda i: (0, 0)),
            pl.BlockSpec((D_MODEL, 3 * BRANCH), lambda i: (0, 0), **const),
            pl.BlockSpec((1, BRANCH), lambda i: (0, 0)),
            pl.BlockSpec((SGU_GROUPS, CHUNK, CHUNK), lambda i: (0, 0, 0), **const),
            pl.BlockSpec((CHUNK, BRANCH), lambda i: (0, 0), **const),
            pl.BlockSpec((BRANCH, D_MODEL), lambda i: (0, 0), **const),
            pl.BlockSpec((1, D_MODEL), lambda i: (0, 0)),
        ],
        out_specs=pl.BlockSpec((rows, D_MODEL), lambda i: (i, 0)),
        out_shape=jax.ShapeDtypeStruct((tokens, D_MODEL), jnp.float32),
        scratch_shapes=[pltpu.VMEM((rows, BRANCH), jnp.float32)],
        compiler_params=pltpu.CompilerParams(dimension_semantics=("arbitrary",),
                                             vmem_limit_bytes=V7X_VMEM_LIMIT),
        name="sgu_layer",
    )(x2d, gain, w_in, v_gain, w_s, b_full, w_out, fin_gain)


def kernel(x, rel_bias, attn_norm, attn_w_in, attn_lam_q1, attn_lam_k1, attn_lam_q2, attn_lam_k2,
           attn_subln, attn_w_out, sgu_norm, sgu_w_in, sgu_v_norm, sgu_w_s, sgu_b_s, sgu_w_out,
           final_norm):
    batch, seq, _ = x.shape
    assert seq % ATTN_BLOCK == 0 and (batch * seq) % OUT_ROWS == 0
    depth = attn_w_in.shape[0] + sgu_w_in.shape[0]
    bf16 = jnp.bfloat16
    x2d = x.reshape(batch * seq, D_MODEL)
    bias = _bias_tiles(rel_bias)
    row = lambda a: a.reshape(1, -1)
    fin = row(final_norm)

    for i in range(depth):
        j = i // 2
        if i % 2 == 0:
            lam_init = 0.8 - 0.6 * math.exp(-0.3 * i)
            w_in = attn_w_in[j]
            wqkg = jnp.concatenate([w_in[:, :2 * BRANCH], w_in[:, 3 * BRANCH:]], axis=1).astype(bf16)
            wvt = w_in[:, 2 * BRANCH:3 * BRANCH].T.astype(bf16)
            q, k, gate, vt = _attn_proj(x2d, row(attn_norm[j]), wqkg, wvt)
            y = _diff_attn(rel_bias, row(attn_lam_q1[j]), row(attn_lam_k1[j]), row(attn_lam_q2[j]),
                           row(attn_lam_k2[j]), row(attn_subln[j]), q, k, vt, gate, bias,
                           lam_init=lam_init, batch=batch, seq=seq)
            x2d = _out_proj(x2d, y, attn_w_out[j].astype(bf16))
        else:
            b_full = jnp.repeat(sgu_b_s[j].T, BRANCH // SGU_GROUPS, axis=1)
            x2d = _sgu_layer(x2d, row(sgu_norm[j]), sgu_w_in[j].astype(bf16), row(sgu_v_norm[j]),
                             sgu_w_s[j], b_full, sgu_w_out[j].astype(bf16), fin,
                             final_norm=(i == depth - 1))
    return x2d.reshape(batch, seq, D_MODEL)
```

```python
import functools
import math

import jax
import jax.numpy as jnp
import numpy as np
from jax import lax
from jax.experimental import pallas as pl
from jax.experimental.pallas import tpu as pltpu

D_MODEL = 1024
N_HEADS = 16
QK_HALF = 64
HEAD_W = 2 * QK_HALF
BRANCH = N_HEADS * HEAD_W
CHUNK = 128
SGU_GROUPS = 16
N_BUCKETS = 32
MAX_EXACT = N_BUCKETS // 2
MAX_DISTANCE = 128
EPS = 1e-6
NEG = -1e30
LOG2E = math.log2(math.e)

V7X_VMEM_LIMIT = 56 * 1024 * 1024
SUB = 128

ATTN_BLOCK = 512
PROJ_ROWS = 256
OUT_ROWS = 512


def _t5_bucket_np(n):
    n = np.asarray(n, dtype=np.int32)
    nf = np.maximum(n, 1).astype(np.float32)
    ratio = np.log(nf / np.float32(MAX_EXACT)) / np.float32(math.log(MAX_DISTANCE / MAX_EXACT))
    large = MAX_EXACT + (ratio * np.float32(N_BUCKETS - MAX_EXACT)).astype(np.int32)
    large = np.minimum(large, N_BUCKETS - 1)
    return np.where(n < MAX_EXACT, n, large).astype(np.int32)


def _bucket_subtiles():
    r = np.arange(SUB, dtype=np.int32)[:, None]
    c = np.arange(SUB, dtype=np.int32)[None, :]
    diag = np.where(c - r >= 0, _t5_bucket_np(np.maximum(c - r, 0)), -1).astype(np.int32)
    nxt = _t5_bucket_np(SUB + c - r)
    assert (_t5_bucket_np(np.arange(SUB + 1, 8192)) == N_BUCKETS - 1).all()
    return diag, nxt


def _rms_rows(x, gain):
    return x * lax.rsqrt(jnp.mean(x * x, axis=-1, keepdims=True) + EPS) * gain


def _silu(x):
    return x * jax.nn.sigmoid(x)


def _bias_tile_kernel(tbl_ref, diag_ref, nxt_ref, out_ref, corner_ref):
    h = pl.program_id(0)
    diag_b = diag_ref[...]
    nxt_b = nxt_ref[...]
    e0 = jnp.zeros((SUB, SUB), jnp.float32)
    e1 = jnp.zeros((SUB, SUB), jnp.float32)
    for b in range(N_BUCKETS):
        t = tbl_ref[b, h] * LOG2E
        e0 = jnp.where(diag_b == b, t, e0)
        e1 = jnp.where(nxt_b == b, t, e1)
    e0 = jnp.where(diag_b < 0, NEG, e0)
    far = jnp.full((SUB, SUB), tbl_ref[N_BUCKETS - 1, h] * LOG2E, jnp.float32)
    masked = jnp.full((SUB, SUB), NEG, jnp.float32)
    n_sub = ATTN_BLOCK // SUB
    for a in range(n_sub):
        for b in range(n_sub):
            delta = b - a
            tile = masked if delta < 0 else e0 if delta == 0 else e1 if delta == 1 else far
            out_ref[0, a * SUB:(a + 1) * SUB, b * SUB:(b + 1) * SUB] = tile
    corner_ref[0] = e1 - far


def _bias_tiles(rel_bias):
    diag, nxt = _bucket_subtiles()
    return pl.pallas_call(
        _bias_tile_kernel,
        grid=(N_HEADS,),
        in_specs=[
            pl.BlockSpec(memory_space=pltpu.SMEM),
            pl.BlockSpec((SUB, SUB), lambda h: (0, 0)),
            pl.BlockSpec((SUB, SUB), lambda h: (0, 0)),
        ],
        out_specs=[pl.BlockSpec((1, ATTN_BLOCK, ATTN_BLOCK), lambda h: (h, 0, 0)),
                   pl.BlockSpec((1, SUB, SUB), lambda h: (h, 0, 0))],
        out_shape=[jax.ShapeDtypeStruct((N_HEADS, ATTN_BLOCK, ATTN_BLOCK), jnp.float32),
                   jax.ShapeDtypeStruct((N_HEADS, SUB, SUB), jnp.float32)],
        compiler_params=pltpu.CompilerParams(dimension_semantics=("arbitrary",)),
        name="bias_tiles",
    )(rel_bias, jnp.asarray(diag), jnp.asarray(nxt))


def _attn_proj_kernel(x_ref, g_ref, wqkg_ref, wvt_ref, q_ref, k_ref, gate_ref, vt_ref):
    h = _rms_rows(x_ref[...], g_ref[...]).astype(jnp.bfloat16)
    scale = QK_HALF ** -0.5 * LOG2E
    for out_ref, col, mul in ((q_ref, 0, scale), (k_ref, BRANCH, None), (gate_ref, 2 * BRANCH, None)):
        for half in range(2):
            lo = half * (BRANCH // 2)
            acc = jnp.dot(h, wqkg_ref[:, col + lo:col + lo + BRANCH // 2],
                          preferred_element_type=jnp.float32)
            if mul is not None:
                acc = acc * mul
            out_ref[:, lo:lo + BRANCH // 2] = acc.astype(out_ref.dtype)
    vt = lax.dot_general(wvt_ref[...], h, (((1,), (1,)), ((), ())),
                         preferred_element_type=jnp.float32)
    vt_ref[...] = vt.astype(vt_ref.dtype)


def _attn_proj(x2d, gain, wqkg, wvt):
    tokens = x2d.shape[0]
    rows = PROJ_ROWS
    const = dict(pipeline_mode=pl.Buffered(1))
    out_bf16 = jax.ShapeDtypeStruct((tokens, BRANCH), jnp.bfloat16)
    return pl.pallas_call(
        _attn_proj_kernel,
        grid=(tokens // rows,),
        in_specs=[
            pl.BlockSpec((rows, D_MODEL), lambda i: (i, 0)),
            pl.BlockSpec((1, D_MODEL), lambda i: (0, 0)),
            pl.BlockSpec((D_MODEL, 3 * BRANCH), lambda i: (0, 0), **const),
            pl.BlockSpec((BRANCH, D_MODEL), lambda i: (0, 0), **const),
        ],
        out_specs=[
            pl.BlockSpec((rows, BRANCH), lambda i: (i, 0)),
            pl.BlockSpec((rows, BRANCH), lambda i: (i, 0)),
            pl.BlockSpec((rows, BRANCH), lambda i: (i, 0)),
            pl.BlockSpec((BRANCH, rows), lambda i: (0, i)),
        ],
        out_shape=[out_bf16, out_bf16, out_bf16,
                   jax.ShapeDtypeStruct((BRANCH, tokens), jnp.bfloat16)],
        compiler_params=pltpu.CompilerParams(dimension_semantics=("arbitrary",),
                                             vmem_limit_bytes=V7X_VMEM_LIMIT),
        name="attn_proj",
    )(x2d, gain, wqkg, wvt)


def _diff_attn_kernel(tbl_ref, lq1_ref, lk1_ref, lq2_ref, lk2_ref, subg_ref,
                      q_ref, k_ref, vt_ref, gate_ref, bias_ref, corner_ref, y_ref,
                      m_ref, l_ref, alpha_ref, acc_ref, s_ref, p_ref, *, lam_init, seq):
    blk = ATTN_BLOCK
    nq = seq // blk
    h = pl.program_id(0)
    far_bias = tbl_ref[N_BUCKETS - 1, h] * LOG2E
    lam = (jnp.exp(jnp.sum(lq1_ref[...] * lk1_ref[...], axis=-1, keepdims=True))
           - jnp.exp(jnp.sum(lq2_ref[...] * lk2_ref[...], axis=-1, keepdims=True)) + lam_init)
    first_half = lax.broadcasted_iota(jnp.int32, (1, HEAD_W), 1) < QK_HALF

    def aligned(start):
        return start if isinstance(start, int) else pl.multiple_of(start, blk)

    def q_operands(qi):
        qb = q_ref[0, pl.ds(aligned(qi * blk), blk), :]
        zero = jnp.zeros_like(qb)
        return jnp.where(first_half, qb, zero), jnp.where(first_half, zero, qb)

    def start_scores(q_maps, k0):
        kb = k_ref[0, pl.ds(aligned(k0), blk), :]
        for mp in range(2):
            s_ref[mp] = lax.dot_general(kb, q_maps[mp], (((1,), (1,)), ((), ())),
                                        preferred_element_type=jnp.float32)

    def value_matmul(mp, k0, p):
        vtb = vt_ref[:, pl.ds(aligned(k0), blk)]
        return alpha_ref[mp] * acc_ref[mp] + jnp.dot(vtb, p, preferred_element_type=jnp.float32)

    def finish_block(k0):
        for mp in range(2):
            acc_ref[mp] = value_matmul(mp, k0, p_ref[mp])

    def softmax(bias, shift):
        probs = []
        for mp in range(2):
            s = s_ref[mp]
            if bias is not None:
                s = s + bias
            m_old = m_ref[mp]
            m_new = jnp.maximum(m_old, jnp.max(s, axis=0, keepdims=True) + shift)
            alpha = jnp.exp2(m_old - m_new)
            p = jnp.exp2(s - (m_new - shift))
            l_ref[mp] = alpha * l_ref[mp] + jnp.sum(p, axis=0, keepdims=True)
            m_ref[mp] = m_new
            alpha_ref[mp] = alpha
            probs.append(p.astype(jnp.bfloat16))
        return probs

    def reset_state():
        m_ref[...] = jnp.full(m_ref.shape, NEG, jnp.float32)
        l_ref[...] = jnp.zeros(l_ref.shape, jnp.float32)
        acc_ref[...] = jnp.zeros(acc_ref.shape, jnp.float32)

    def kv_step(prev_k0, near, q_maps, next_k0):
        if prev_k0 is not None:
            finish_block(prev_k0)
        if near:
            for mp in range(2):
                s_ref[mp, blk - SUB:, :SUB] = s_ref[mp, blk - SUB:, :SUB] + corner_ref[0]
        probs = softmax(None, far_bias)
        for mp in range(2):
            p_ref[mp] = probs[mp]
        start_scores(q_maps, next_k0)

    def diag_step(qi, prev_k0):
        q0 = aligned(qi * blk)
        if prev_k0 is not None:
            finish_block(prev_k0)
        probs = softmax(bias_ref[0], 0.0)
        start_scores(q_operands(jnp.minimum(qi + 1, nq - 1)), 0)
        o = [value_matmul(mp, q0, probs[mp]) / l_ref[mp] for mp in range(2)]
        o_t = o[0] - lam * o[1]
        out = _rms_rows(o_t.T, subg_ref[...]) * (1.0 - lam_init)
        gate = gate_ref[0, pl.ds(q0, blk), :].astype(jnp.float32)
        y_ref[0, pl.ds(q0, blk), :] = (out * _silu(gate)).astype(y_ref.dtype)
        reset_state()

    reset_state()
    start_scores(q_operands(0), 0)
    diag_step(0, None)
    if nq >= 2:
        kv_step(None, True, q_operands(1), blk)
        diag_step(1, 0)

    def q_block(qi, carry):
        q_maps = q_operands(qi)
        kv_step(None, False, q_maps, blk)

        def far_step(j, c):
            k0 = j * blk
            kv_step(k0 - blk, False, q_maps, k0 + blk)
            return c

        lax.fori_loop(1, qi - 1, far_step, 0)
        kp = (qi - 1) * blk
        kv_step(kp - blk, True, q_maps, kp + blk)
        diag_step(qi, kp)
        return carry

    lax.fori_loop(2, nq, q_block, 0)


def _diff_attn(rel_bias, lq1, lk1, lq2, lk2, subg, q, k, vt, gate, bias, corner, *, lam_init, batch, seq):
    blk = ATTN_BLOCK
    q3 = q.reshape(batch, seq, BRANCH)
    k3 = k.reshape(batch, seq, BRANCH)
    g3 = gate.reshape(batch, seq, BRANCH)
    vec = lambda n: pl.BlockSpec((1, n), lambda h, b: (0, 0))
    head = pl.BlockSpec((1, seq, HEAD_W), lambda h, b: (b, 0, h))
    y = pl.pallas_call(
        functools.partial(_diff_attn_kernel, lam_init=lam_init, seq=seq),
        grid=(N_HEADS, batch),
        in_specs=[
            pl.BlockSpec(memory_space=pltpu.SMEM),
            vec(QK_HALF), vec(QK_HALF), vec(QK_HALF), vec(QK_HALF), vec(HEAD_W),
            head, head,
            pl.BlockSpec((HEAD_W, seq), lambda h, b: (h, b)),
            head,
            pl.BlockSpec((1, blk, blk), lambda h, b: (h, 0, 0)),
            pl.BlockSpec((1, SUB, SUB), lambda h, b: (h, 0, 0)),
        ],
        out_specs=head,
        out_shape=jax.ShapeDtypeStruct((batch, seq, BRANCH), jnp.bfloat16),
        scratch_shapes=[
            pltpu.VMEM((2, 1, blk), jnp.float32),
            pltpu.VMEM((2, 1, blk), jnp.float32),
            pltpu.VMEM((2, 1, blk), jnp.float32),
            pltpu.VMEM((2, HEAD_W, blk), jnp.float32),
            pltpu.VMEM((2, blk, blk), jnp.float32),
            pltpu.VMEM((2, blk, blk), jnp.bfloat16),
        ],
        compiler_params=pltpu.CompilerParams(dimension_semantics=("arbitrary", "arbitrary"),
                                             vmem_limit_bytes=V7X_VMEM_LIMIT),
        name="diff_attn",
    )(rel_bias, lq1, lk1, lq2, lk2, subg, q3, k3, vt, g3, bias, corner)
    return y.reshape(batch * seq, BRANCH)


def _out_proj_kernel(x_ref, y_ref, w_ref, o_ref):
    o_ref[...] = x_ref[...] + jnp.dot(y_ref[...], w_ref[...], preferred_element_type=jnp.float32)


def _out_proj(x2d, y, w_out):
    tokens = x2d.shape[0]
    rows = OUT_ROWS
    return pl.pallas_call(
        _out_proj_kernel,
        grid=(tokens // rows,),
        in_specs=[
            pl.BlockSpec((rows, D_MODEL), lambda i: (i, 0)),
            pl.BlockSpec((rows, BRANCH), lambda i: (i, 0)),
            pl.BlockSpec((BRANCH, D_MODEL), lambda i: (0, 0), pipeline_mode=pl.Buffered(1)),
        ],
        out_specs=pl.BlockSpec((rows, D_MODEL), lambda i: (i, 0)),
        out_shape=jax.ShapeDtypeStruct((tokens, D_MODEL), jnp.float32),
        compiler_params=pltpu.CompilerParams(dimension_semantics=("arbitrary",),
                                             vmem_limit_bytes=V7X_VMEM_LIMIT),
        name="attn_out_proj",
    )(x2d, y, w_out)


def _sgu_kernel(x_ref, g_ref, win_ref, vgain_ref, ws_ref, bs_ref, wout_ref, fin_ref, o_ref,
                y_scr, *, final_norm):
    x = x_ref[...]
    rows = x.shape[0]
    h = _rms_rows(x, g_ref[...]).astype(jnp.bfloat16)
    u = jnp.dot(h, win_ref[:, :BRANCH], preferred_element_type=jnp.float32)
    v = jnp.dot(h, win_ref[:, BRANCH:2 * BRANCH], preferred_element_type=jnp.float32)
    gate = jnp.dot(h, win_ref[:, 2 * BRANCH:], preferred_element_type=jnp.float32)
    v = _rms_rows(v, vgain_ref[...]).astype(jnp.bfloat16)

    t_idx = lax.broadcasted_iota(jnp.int32, (CHUNK, CHUNK), 0)
    s_idx = lax.broadcasted_iota(jnp.int32, (CHUNK, CHUNK), 1)
    causal = s_idx <= t_idx
    gw = BRANCH // SGU_GROUPS
    for g in range(SGU_GROUPS):
        w = jnp.where(causal, ws_ref[g], 0.0).astype(jnp.bfloat16)
        b = bs_ref[:, g * gw:(g + 1) * gw]
        for c in range(rows // CHUNK):
            vc = v[c * CHUNK:(c + 1) * CHUNK, g * gw:(g + 1) * gw]
            y_scr[c * CHUNK:(c + 1) * CHUNK, g * gw:(g + 1) * gw] = (
                jnp.dot(w, vc, preferred_element_type=jnp.float32) + b)

    z = (u * y_scr[...] * _silu(gate)).astype(jnp.bfloat16)
    out = x + jnp.dot(z, wout_ref[...], preferred_element_type=jnp.float32)
    if final_norm:
        out = _rms_rows(out, fin_ref[...])
    o_ref[...] = out


def _sgu_layer(x2d, gain, w_in, v_gain, w_s, b_full, w_out, fin_gain, *, final_norm):
    tokens = x2d.shape[0]
    rows = PROJ_ROWS
    const = dict(pipeline_mode=pl.Buffered(1))
    return pl.pallas_call(
        functools.partial(_sgu_kernel, final_norm=final_norm),
        grid=(tokens // rows,),
        in_specs=[
            pl.BlockSpec((rows, D_MODEL), lambda i: (i, 0)),
            pl.BlockSpec((1, D_MODEL), lambda i: (0, 0)),
            pl.BlockSpec((D_MODEL, 3 * BRANCH), lambda i: (0, 0), **const),
            pl.BlockSpec((1, BRANCH), lambda i: (0, 0)),
            pl.BlockSpec((SGU_GROUPS, CHUNK, CHUNK), lambda i: (0, 0, 0), **const),
            pl.BlockSpec((CHUNK, BRANCH), lambda i: (0, 0), **const),
            pl.BlockSpec((BRANCH, D_MODEL), lambda i: (0, 0), **const),
            pl.BlockSpec((1, D_MODEL), lambda i: (0, 0)),
        ],
        out_specs=pl.BlockSpec((rows, D_MODEL), lambda i: (i, 0)),
        out_shape=jax.ShapeDtypeStruct((tokens, D_MODEL), jnp.float32),
        scratch_shapes=[pltpu.VMEM((rows, BRANCH), jnp.float32)],
        compiler_params=pltpu.CompilerParams(dimension_semantics=("arbitrary",),
                                             vmem_limit_bytes=V7X_VMEM_LIMIT),
        name="sgu_layer",
    )(x2d, gain, w_in, v_gain, w_s, b_full, w_out, fin_gain)


def kernel(x, rel_bias, attn_norm, attn_w_in, attn_lam_q1, attn_lam_k1, attn_lam_q2, attn_lam_k2,
           attn_subln, attn_w_out, sgu_norm, sgu_w_in, sgu_v_norm, sgu_w_s, sgu_b_s, sgu_w_out,
           final_norm):
    batch, seq, _ = x.shape
    assert seq % ATTN_BLOCK == 0 and (batch * seq) % OUT_ROWS == 0
    depth = attn_w_in.shape[0] + sgu_w_in.shape[0]
    bf16 = jnp.bfloat16
    x2d = x.reshape(batch * seq, D_MODEL)
    bias, corner = _bias_tiles(rel_bias)
    row = lambda a: a.reshape(1, -1)
    fin = row(final_norm)

    for i in range(depth):
        j = i // 2
        if i % 2 == 0:
            lam_init = 0.8 - 0.6 * math.exp(-0.3 * i)
            w_in = attn_w_in[j]
            wqkg = jnp.concatenate([w_in[:, :2 * BRANCH], w_in[:, 3 * BRANCH:]], axis=1).astype(bf16)
            wvt = w_in[:, 2 * BRANCH:3 * BRANCH].T.astype(bf16)
            q, k, gate, vt = _attn_proj(x2d, row(attn_norm[j]), wqkg, wvt)
            y = _diff_attn(rel_bias, row(attn_lam_q1[j]), row(attn_lam_k1[j]), row(attn_lam_q2[j]),
                           row(attn_lam_k2[j]), row(attn_subln[j]), q, k, vt, gate, bias, corner,
                           lam_init=lam_init, batch=batch, seq=seq)
            x2d = _out_proj(x2d, y, attn_w_out[j].astype(bf16))
        else:
            b_full = jnp.repeat(sgu_b_s[j].T, BRANCH // SGU_GROUPS, axis=1)
            x2d = _sgu_layer(x2d, row(sgu_norm[j]), sgu_w_in[j].astype(bf16), row(sgu_v_norm[j]),
                             sgu_w_s[j], b_full, sgu_w_out[j].astype(bf16), fin,
                             final_norm=(i == depth - 1))
    return x2d.reshape(batch, seq, D_MODEL)
```

```python
import functools
import math

import jax
import jax.numpy as jnp
import numpy as np
from jax import lax
from jax.experimental import pallas as pl
from jax.experimental.pallas import tpu as pltpu

D_MODEL = 1024
N_HEADS = 16
QK_HALF = 64
HEAD_W = 2 * QK_HALF
BRANCH = N_HEADS * HEAD_W
CHUNK = 128
SGU_GROUPS = 16
N_BUCKETS = 32
MAX_EXACT = N_BUCKETS // 2
MAX_DISTANCE = 128
EPS = 1e-6
NEG = -1e30
LOG2E = math.log2(math.e)

V7X_VMEM_LIMIT = 56 * 1024 * 1024
SUB = 128

ATTN_BLOCK = 512
FAR, NEAR, DIAG = range(3)
QUERY_CHUNK = 256
PROJ_ROWS = 512
SGU_ROWS = 512


def _t5_bucket_np(n):
    n = np.asarray(n, dtype=np.int32)
    nf = np.maximum(n, 1).astype(np.float32)
    ratio = np.log(nf / np.float32(MAX_EXACT)) / np.float32(math.log(MAX_DISTANCE / MAX_EXACT))
    large = MAX_EXACT + (ratio * np.float32(N_BUCKETS - MAX_EXACT)).astype(np.int32)
    large = np.minimum(large, N_BUCKETS - 1)
    return np.where(n < MAX_EXACT, n, large).astype(np.int32)


def _bucket_subtiles():
    r = np.arange(SUB, dtype=np.int32)[:, None]
    c = np.arange(SUB, dtype=np.int32)[None, :]
    diag = np.where(c - r >= 0, _t5_bucket_np(np.maximum(c - r, 0)), -1).astype(np.int32)
    nxt = _t5_bucket_np(SUB + c - r)
    assert (_t5_bucket_np(np.arange(SUB + 1, 8192)) == N_BUCKETS - 1).all()
    return diag, nxt


def _rms_rows(x, gain):
    return x * lax.rsqrt(jnp.mean(x * x, axis=-1, keepdims=True) + EPS) * gain


def _silu(x):
    return x * jax.nn.sigmoid(x)


def _bias_tile_kernel(tbl_ref, diag_ref, nxt_ref, out_ref, near_ref):
    h = pl.program_id(0)
    diag_b = diag_ref[...]
    nxt_b = nxt_ref[...]
    e0 = jnp.zeros((SUB, SUB), jnp.float32)
    e1 = jnp.zeros((SUB, SUB), jnp.float32)
    for b in range(N_BUCKETS):
        t = tbl_ref[b, h] * LOG2E
        e0 = jnp.where(diag_b == b, t, e0)
        e1 = jnp.where(nxt_b == b, t, e1)
    e0 = jnp.where(diag_b < 0, NEG, e0)
    far = jnp.full((SUB, SUB), tbl_ref[N_BUCKETS - 1, h] * LOG2E, jnp.float32)
    masked = jnp.full((SUB, SUB), NEG, jnp.float32)
    n_sub = ATTN_BLOCK // SUB
    for a in range(n_sub):
        for b in range(n_sub):
            delta = b - a
            tile = masked if delta < 0 else e0 if delta == 0 else e1 if delta == 1 else far
            out_ref[0, a * SUB:(a + 1) * SUB, b * SUB:(b + 1) * SUB] = tile
    near_ref[0] = jnp.zeros(near_ref.shape[1:], jnp.float32)
    near_ref[0, ATTN_BLOCK - SUB:, :SUB] = e1 - far


def _bias_tiles(rel_bias):
    diag, nxt = _bucket_subtiles()
    return pl.pallas_call(
        _bias_tile_kernel,
        grid=(N_HEADS,),
        in_specs=[
            pl.BlockSpec(memory_space=pltpu.SMEM),
            pl.BlockSpec((SUB, SUB), lambda h: (0, 0)),
            pl.BlockSpec((SUB, SUB), lambda h: (0, 0)),
        ],
        out_specs=[pl.BlockSpec((1, ATTN_BLOCK, ATTN_BLOCK), lambda h: (h, 0, 0)),
                   pl.BlockSpec((1, ATTN_BLOCK, QUERY_CHUNK), lambda h: (h, 0, 0))],
        out_shape=[jax.ShapeDtypeStruct((N_HEADS, ATTN_BLOCK, ATTN_BLOCK), jnp.float32),
                   jax.ShapeDtypeStruct((N_HEADS, ATTN_BLOCK, QUERY_CHUNK), jnp.float32)],
        compiler_params=pltpu.CompilerParams(dimension_semantics=("arbitrary",)),
        name="bias_tiles",
    )(rel_bias, jnp.asarray(diag), jnp.asarray(nxt))


def _attn_proj_kernel(x_ref, g_ref, wqkg_ref, wvt_ref, q_ref, k_ref, gate_ref, vt_ref):
    h = _rms_rows(x_ref[...], g_ref[...]).astype(jnp.bfloat16)
    scale = QK_HALF ** -0.5 * LOG2E
    for out_ref, col, mul in ((q_ref, 0, scale), (k_ref, BRANCH, None), (gate_ref, 2 * BRANCH, None)):
        for half in range(2):
            lo = half * (BRANCH // 2)
            acc = jnp.dot(h, wqkg_ref[:, col + lo:col + lo + BRANCH // 2],
                          preferred_element_type=jnp.float32)
            if mul is not None:
                acc = acc * mul
            out_ref[:, lo:lo + BRANCH // 2] = acc.astype(out_ref.dtype)
    vt = lax.dot_general(wvt_ref[...], h, (((1,), (1,)), ((), ())),
                         preferred_element_type=jnp.float32)
    vt_ref[...] = vt.astype(vt_ref.dtype)


def _attn_proj(x2d, gain, wqkg, wvt):
    tokens = x2d.shape[0]
    rows = PROJ_ROWS
    const = dict(pipeline_mode=pl.Buffered(1))
    out_bf16 = jax.ShapeDtypeStruct((tokens, BRANCH), jnp.bfloat16)
    return pl.pallas_call(
        _attn_proj_kernel,
        grid=(tokens // rows,),
        in_specs=[
            pl.BlockSpec((rows, D_MODEL), lambda i: (i, 0)),
            pl.BlockSpec((1, D_MODEL), lambda i: (0, 0)),
            pl.BlockSpec((D_MODEL, 3 * BRANCH), lambda i: (0, 0), **const),
            pl.BlockSpec((BRANCH, D_MODEL), lambda i: (0, 0), **const),
        ],
        out_specs=[
            pl.BlockSpec((rows, BRANCH), lambda i: (i, 0)),
            pl.BlockSpec((rows, BRANCH), lambda i: (i, 0)),
            pl.BlockSpec((rows, BRANCH), lambda i: (i, 0)),
            pl.BlockSpec((BRANCH, rows), lambda i: (0, i)),
        ],
        out_shape=[out_bf16, out_bf16, out_bf16,
                   jax.ShapeDtypeStruct((BRANCH, tokens), jnp.bfloat16)],
        compiler_params=pltpu.CompilerParams(dimension_semantics=("arbitrary",),
                                             vmem_limit_bytes=V7X_VMEM_LIMIT),
        name="attn_proj",
    )(x2d, gain, wqkg, wvt)


def _diff_attn_kernel(tbl_ref, lq1_ref, lk1_ref, lq2_ref, lk2_ref, subg_ref,
                      q_ref, k_ref, vt_ref, gate_ref, bias_ref, near_ref, y_ref,
                      sub_ref, l_ref, acc_ref, s_ref, p_ref, bad_ref, redo_max_ref, redo_l_ref, redo_acc_ref,
                      *, lam_init, seq):
    blk = ATTN_BLOCK
    nq = seq // blk
    chunks = range(blk // QUERY_CHUNK)
    h = pl.program_id(0)
    far_bias = tbl_ref[N_BUCKETS - 1, h] * LOG2E
    lam = (jnp.exp(jnp.sum(lq1_ref[...] * lk1_ref[...], axis=-1, keepdims=True))
           - jnp.exp(jnp.sum(lq2_ref[...] * lk2_ref[...], axis=-1, keepdims=True)) + lam_init)
    first_half = lax.broadcasted_iota(jnp.int32, (1, HEAD_W), 1) < QK_HALF

    def aligned(start):
        return start if isinstance(start, int) else pl.multiple_of(start, blk)

    def q_operands(qi):
        qb = q_ref[0, pl.ds(aligned(qi * blk), blk), :]
        zero = jnp.zeros_like(qb)
        return jnp.where(first_half, qb, zero), jnp.where(first_half, zero, qb)

    def cols(c):
        return slice(c * QUERY_CHUNK, (c + 1) * QUERY_CHUNK)

    def live_keys(c, kind):
        return (c + 1) * QUERY_CHUNK if kind == DIAG else blk

    def scores(c, q_op, k0, kind):
        nk = live_keys(c, kind)
        kb = k_ref[0, pl.ds(aligned(k0), nk), :]
        s = lax.dot_general(kb, q_op[cols(c), :], (((1,), (1,)), ((), ())),
                            preferred_element_type=jnp.float32)
        if kind == DIAG:
            s = s + bias_ref[0, :nk, cols(c)]
        if kind == NEAR and c == 0:
            s = s + near_ref[0]
        return s

    def subtrahend(mp, c, kind):
        sub = sub_ref[mp, :, cols(c)]
        return sub + far_bias if kind == DIAG else sub

    def first_scores(q_ops, kind):
        for mp in range(2):
            for c in chunks:
                nk = live_keys(c, kind)
                s = scores(c, q_ops[mp], 0, kind)
                s_ref[mp, :nk, cols(c)] = s
                top = jnp.max(s, axis=0, keepdims=True)
                sub_ref[mp, :, cols(c)] = top - far_bias if kind == DIAG else top

    def first_probabilities(kind):
        acc_ref[1] = jnp.zeros(acc_ref.shape[1:], jnp.float32)
        for mp in range(2):
            for c in chunks:
                nk = live_keys(c, kind)
                p = jnp.exp2(s_ref[mp, :nk, cols(c)] - subtrahend(mp, c, kind))
                l_ref[mp, :, cols(c)] = jnp.sum(p, axis=0, keepdims=True)
                p_ref[mp, :nk, cols(c)] = p.astype(jnp.bfloat16)

    def probabilities(mp, q_op, k0, kind):
        for c in chunks:
            nk = live_keys(c, kind)
            p = jnp.exp2(scores(c, q_op, k0, kind) - subtrahend(mp, c, kind))
            l_ref[mp, :, cols(c)] = l_ref[mp, :, cols(c)] + jnp.sum(p, axis=0, keepdims=True)
            p_ref[mp, :nk, cols(c)] = p.astype(jnp.bfloat16)

    def value_matmul(mp, k0, kind, init):
        for c in chunks:
            nk = live_keys(c, kind)
            vtb = vt_ref[:, pl.ds(aligned(k0), nk)]
            pv = jnp.dot(vtb, p_ref[mp, :nk, cols(c)], preferred_element_type=jnp.float32)
            acc_ref[mp, :, cols(c)] = pv if init else acc_ref[mp, :, cols(c)] + pv

    def step(q_ops, k0, kind, prev_k0):
        probabilities(0, q_ops[0], k0, kind)
        value_matmul(1, prev_k0, FAR, False)
        probabilities(1, q_ops[1], k0, kind)
        value_matmul(0, k0, kind, False)

    def write_out(q0, num_ref, den_ref):
        o_t = num_ref[0] / den_ref[0] - lam * (num_ref[1] / den_ref[1])
        out = _rms_rows(o_t.T, subg_ref[...]) * (1.0 - lam_init)
        gate = gate_ref[0, pl.ds(q0, blk), :].astype(jnp.float32)
        y_ref[0, pl.ds(q0, blk), :] = (out * _silu(gate)).astype(y_ref.dtype)
        return o_t

    def redo_q_block(qi):
        q_ops = q_operands(qi)
        redo_max_ref[...] = jnp.full(redo_max_ref.shape, NEG, jnp.float32)
        redo_l_ref[...] = jnp.zeros(redo_l_ref.shape, jnp.float32)
        redo_acc_ref[...] = jnp.zeros(redo_acc_ref.shape, jnp.float32)

        def block(k0, kind):
            for mp in range(2):
                for c in chunks:
                    nk = live_keys(c, kind)
                    s = scores(c, q_ops[mp], k0, kind)
                    if kind != DIAG:
                        s = s + far_bias
                    m_old = redo_max_ref[mp, :, cols(c)]
                    m_new = jnp.maximum(m_old, jnp.max(s, axis=0, keepdims=True))
                    alpha = jnp.exp2(m_old - m_new)
                    p = jnp.exp2(s - m_new)
                    redo_l_ref[mp, :, cols(c)] = (alpha * redo_l_ref[mp, :, cols(c)]
                                                  + jnp.sum(p, axis=0, keepdims=True))
                    vtb = vt_ref[:, pl.ds(aligned(k0), nk)]
                    pv = jnp.dot(vtb, p.astype(jnp.bfloat16), preferred_element_type=jnp.float32)
                    redo_acc_ref[mp, :, cols(c)] = alpha * redo_acc_ref[mp, :, cols(c)] + pv
                    redo_max_ref[mp, :, cols(c)] = m_new

        def far_block(j, carry):
            block(j * blk, FAR)
            return carry

        lax.fori_loop(0, qi - 1, far_block, 0)

        @pl.when(jnp.asarray(qi, jnp.int32) >= 1)
        def _():
            block((qi - 1) * blk, NEAR)

        block(qi * blk, DIAG)
        write_out(aligned(qi * blk), redo_acc_ref, redo_l_ref)

    def finish_q_block(qi):
        o_t = write_out(aligned(qi * blk), acc_ref, l_ref)
        finite = jnp.isfinite(o_t) & jnp.isfinite(l_ref[0]) & jnp.isfinite(l_ref[1])
        bad = jnp.max(jnp.where(finite, 0.0, 1.0), axis=0, keepdims=True)
        bad_ref[...] = jnp.maximum(bad_ref[...], bad)

    def open_q_block(kind):
        first_probabilities(kind)
        value_matmul(0, 0, kind, True)

    bad_ref[...] = jnp.zeros(bad_ref.shape, jnp.float32)
    ops = q_operands(0)
    first_scores(ops, DIAG)
    open_q_block(DIAG)
    value_matmul(1, 0, DIAG, False)
    if nq >= 2:
        ops = q_operands(1)
        first_scores(ops, NEAR)
        finish_q_block(0)
        open_q_block(NEAR)
        step(ops, blk, DIAG, 0)
        value_matmul(1, blk, DIAG, False)
        first_scores(q_operands(min(2, nq - 1)), FAR)
        finish_q_block(1)
        open_q_block(FAR)
    else:
        finish_q_block(0)

    def q_block(qi, carry):
        q_ops = q_operands(qi)

        odd = qi % 2

        @pl.when(odd == 1)
        def _():
            step(q_ops, blk, FAR, 0)

        def far_pair(j, c):
            k0 = (1 + odd + 2 * j) * blk
            step(q_ops, k0, FAR, k0 - blk)
            step(q_ops, k0 + blk, FAR, k0)
            return c

        lax.fori_loop(0, (qi - 2) // 2, far_pair, 0)
        kn = (qi - 1) * blk
        step(q_ops, kn, NEAR, kn - blk)
        step(q_ops, kn + blk, DIAG, kn)
        value_matmul(1, kn + blk, DIAG, False)
        first_scores(q_operands(jnp.minimum(qi + 1, nq - 1)), FAR)
        finish_q_block(qi)
        open_q_block(FAR)
        return carry

    lax.fori_loop(2, nq, q_block, 0)

    @pl.when(jnp.max(bad_ref[...]) > 0.0)
    def _():
        def redo(qi, carry):
            redo_q_block(qi)
            return carry

        lax.fori_loop(0, nq, redo, 0)


def _diff_attn(rel_bias, lq1, lk1, lq2, lk2, subg, q, k, vt, gate, bias, near, *, lam_init, batch, seq):
    blk = ATTN_BLOCK
    q3 = q.reshape(batch, seq, BRANCH)
    k3 = k.reshape(batch, seq, BRANCH)
    g3 = gate.reshape(batch, seq, BRANCH)
    vec = lambda n: pl.BlockSpec((1, n), lambda h, b: (0, 0))
    head = pl.BlockSpec((1, seq, HEAD_W), lambda h, b: (b, 0, h))
    y = pl.pallas_call(
        functools.partial(_diff_attn_kernel, lam_init=lam_init, seq=seq),
        grid=(N_HEADS, batch),
        in_specs=[
            pl.BlockSpec(memory_space=pltpu.SMEM),
            vec(QK_HALF), vec(QK_HALF), vec(QK_HALF), vec(QK_HALF), vec(HEAD_W),
            head, head,
            pl.BlockSpec((HEAD_W, seq), lambda h, b: (h, b)),
            head,
            pl.BlockSpec((1, blk, blk), lambda h, b: (h, 0, 0)),
            pl.BlockSpec((1, blk, QUERY_CHUNK), lambda h, b: (h, 0, 0)),
        ],
        out_specs=head,
        out_shape=jax.ShapeDtypeStruct((batch, seq, BRANCH), jnp.bfloat16),
        scratch_shapes=[
            pltpu.VMEM((2, 1, blk), jnp.float32),
            pltpu.VMEM((2, 1, blk), jnp.float32),
            pltpu.VMEM((2, HEAD_W, blk), jnp.float32),
            pltpu.VMEM((2, blk, blk), jnp.float32),
            pltpu.VMEM((2, blk, blk), jnp.bfloat16),
            pltpu.VMEM((1, blk), jnp.float32),
            pltpu.VMEM((2, 1, blk), jnp.float32),
            pltpu.VMEM((2, 1, blk), jnp.float32),
            pltpu.VMEM((2, HEAD_W, blk), jnp.float32),
        ],
        compiler_params=pltpu.CompilerParams(dimension_semantics=("arbitrary", "arbitrary"),
                                             vmem_limit_bytes=V7X_VMEM_LIMIT),
        name="diff_attn",
    )(rel_bias, lq1, lk1, lq2, lk2, subg, q3, k3, vt, g3, bias, near)
    return y.reshape(batch * seq, BRANCH)


def _sgu_kernel(x_ref, ya_ref, wa_ref, g_ref, win_ref, vgain_ref, ws_ref, bs_ref, wout_ref, fin_ref,
                o_ref, y_scr, *, final_norm):
    x = x_ref[...] + jnp.dot(ya_ref[...], wa_ref[...], preferred_element_type=jnp.float32)
    rows = x.shape[0]
    h = _rms_rows(x, g_ref[...]).astype(jnp.bfloat16)
    u = jnp.dot(h, win_ref[:, :BRANCH], preferred_element_type=jnp.float32)
    v = jnp.dot(h, win_ref[:, BRANCH:2 * BRANCH], preferred_element_type=jnp.float32)
    gate = jnp.dot(h, win_ref[:, 2 * BRANCH:], preferred_element_type=jnp.float32)
    v = _rms_rows(v, vgain_ref[...]).astype(jnp.bfloat16)

    t_idx = lax.broadcasted_iota(jnp.int32, (CHUNK, CHUNK), 0)
    s_idx = lax.broadcasted_iota(jnp.int32, (CHUNK, CHUNK), 1)
    causal = s_idx <= t_idx
    gw = BRANCH // SGU_GROUPS
    for g in range(SGU_GROUPS):
        w = jnp.where(causal, ws_ref[g], 0.0).astype(jnp.bfloat16)
        b = bs_ref[:, g * gw:(g + 1) * gw]
        for c in range(rows // CHUNK):
            vc = v[c * CHUNK:(c + 1) * CHUNK, g * gw:(g + 1) * gw]
            y_scr[c * CHUNK:(c + 1) * CHUNK, g * gw:(g + 1) * gw] = (
                jnp.dot(w, vc, preferred_element_type=jnp.float32) + b)

    z = (u * y_scr[...] * _silu(gate)).astype(jnp.bfloat16)
    out = x + jnp.dot(z, wout_ref[...], preferred_element_type=jnp.float32)
    if final_norm:
        out = _rms_rows(out, fin_ref[...])
    o_ref[...] = out


def _sgu_layer(x2d, y_attn, w_attn_out, gain, w_in, v_gain, w_s, b_full, w_out, fin_gain, *, final_norm):
    tokens = x2d.shape[0]
    rows = SGU_ROWS
    const = dict(pipeline_mode=pl.Buffered(1))
    return pl.pallas_call(
        functools.partial(_sgu_kernel, final_norm=final_norm),
        grid=(tokens // rows,),
        in_specs=[
            pl.BlockSpec((rows, D_MODEL), lambda i: (i, 0)),
            pl.BlockSpec((rows, BRANCH), lambda i: (i, 0)),
            pl.BlockSpec((BRANCH, D_MODEL), lambda i: (0, 0), **const),
            pl.BlockSpec((1, D_MODEL), lambda i: (0, 0)),
            pl.BlockSpec((D_MODEL, 3 * BRANCH), lambda i: (0, 0), **const),
            pl.BlockSpec((1, BRANCH), lambda i: (0, 0)),
            pl.BlockSpec((SGU_GROUPS, CHUNK, CHUNK), lambda i: (0, 0, 0), **const),
            pl.BlockSpec((CHUNK, BRANCH), lambda i: (0, 0), **const),
            pl.BlockSpec((BRANCH, D_MODEL), lambda i: (0, 0), **const),
            pl.BlockSpec((1, D_MODEL), lambda i: (0, 0)),
        ],
        out_specs=pl.BlockSpec((rows, D_MODEL), lambda i: (i, 0)),
        out_shape=jax.ShapeDtypeStruct((tokens, D_MODEL), jnp.float32),
        scratch_shapes=[pltpu.VMEM((rows, BRANCH), jnp.float32)],
        compiler_params=pltpu.CompilerParams(dimension_semantics=("arbitrary",),
                                             vmem_limit_bytes=V7X_VMEM_LIMIT),
        name="sgu_layer",
    )(x2d, y_attn, w_attn_out, gain, w_in, v_gain, w_s, b_full, w_out, fin_gain)


def kernel(x, rel_bias, attn_norm, attn_w_in, attn_lam_q1, attn_lam_k1, attn_lam_q2, attn_lam_k2,
           attn_subln, attn_w_out, sgu_norm, sgu_w_in, sgu_v_norm, sgu_w_s, sgu_b_s, sgu_w_out,
           final_norm):
    batch, seq, _ = x.shape
    assert seq % ATTN_BLOCK == 0 and (batch * seq) % PROJ_ROWS == 0
    depth = attn_w_in.shape[0] + sgu_w_in.shape[0]
    bf16 = jnp.bfloat16
    x2d = x.reshape(batch * seq, D_MODEL)
    bias, near = _bias_tiles(rel_bias)
    row = lambda a: a.reshape(1, -1)
    fin = row(final_norm)

    assert depth % 2 == 0, "every attention layer hands its output projection to the next sgu layer"
    for i in range(depth):
        j = i // 2
        if i % 2 == 0:
            lam_init = 0.8 - 0.6 * math.exp(-0.3 * i)
            w_in = attn_w_in[j]
            wqkg = jnp.concatenate([w_in[:, :2 * BRANCH], w_in[:, 3 * BRANCH:]], axis=1).astype(bf16)
            wvt = w_in[:, 2 * BRANCH:3 * BRANCH].T.astype(bf16)
            q, k, gate, vt = _attn_proj(x2d, row(attn_norm[j]), wqkg, wvt)
            y = _diff_attn(rel_bias, row(attn_lam_q1[j]), row(attn_lam_k1[j]), row(attn_lam_q2[j]),
                           row(attn_lam_k2[j]), row(attn_subln[j]), q, k, vt, gate, bias, near,
                           lam_init=lam_init, batch=batch, seq=seq)
            w_attn_out = attn_w_out[j].astype(bf16)
        else:
            b_full = jnp.broadcast_to(sgu_b_s[j].T[:, :, None],
                                      (CHUNK, SGU_GROUPS, BRANCH // SGU_GROUPS)).reshape(CHUNK, BRANCH)
            x2d = _sgu_layer(x2d, y, w_attn_out, row(sgu_norm[j]), sgu_w_in[j].astype(bf16), row(sgu_v_norm[j]),
                             sgu_w_s[j], b_full, sgu_w_out[j].astype(bf16), fin,
                             final_norm=(i == depth - 1))
    return x2d.reshape(batch, seq, D_MODEL)
```
